```python
import math
import jax, jax.numpy as jnp
from jax import lax
import numpy as np

D_MODEL = 1024
BATCH = 8
SEQ = 4096
DEPTH = 1

D_RNN = D_MODEL
LRU_BLOCK_DIM = 64
LRU_BLOCKS = D_RNN // LRU_BLOCK_DIM
CONV_WIDTH = 4
LRU_C = 8.0
HEAD_DIM = 64
V_DIM = 2 * HEAD_DIM
N_HEADS = D_MODEL // V_DIM
Q_WIDTH = N_HEADS * 2 * HEAD_DIM
ATT_WIDTH = N_HEADS * V_DIM
ROT_DIM = HEAD_DIM // 4
ROPE_THETA = 500000.0
Q_BLOCK = 128
N_EXPERTS = 256
TOP_K = 8
D_EXPERT = 256
D_SHARED = 256
ROUTED_SCALE = 2.5
MOE_BLOCK = 128
EPS = 1e-6
IN_SPLITS = (D_RNN, D_RNN, Q_WIDTH, Q_WIDTH, ATT_WIDTH, D_MODEL, D_MODEL)
IN_WIDTH = 2 * D_RNN + 2 * Q_WIDTH + ATT_WIDTH + 2 * D_MODEL

kernel_name = 'hybrid_rglru_diffattn_moe_adaln'


def _rmsnorm(x, g):
    xf = x.astype(jnp.float32)
    y = xf * lax.rsqrt(jnp.mean(xf * xf, axis=-1, keepdims=True) + EPS)
    return (y * g.astype(jnp.float32)).astype(x.dtype)


def _rotary_partial(x, cos, sin):
    half = ROT_DIM // 2
    cos = cos.astype(x.dtype)
    sin = sin.astype(x.dtype)
    x1 = x[..., :half]
    x2 = x[..., half:ROT_DIM]
    return jnp.concatenate([x1 * cos - x2 * sin, x2 * cos + x1 * sin, x[..., ROT_DIM:]], axis=-1)


def _causal_depthwise_conv(x, w, b):
    y = lax.conv_general_dilated(
        x, w[:, None, :].astype(x.dtype), window_strides=(1,),
        padding=[(CONV_WIDTH - 1, 0)], dimension_numbers=('NWC', 'WIO', 'NWC'),
        feature_group_count=x.shape[-1])
    return y + b.astype(x.dtype)


def _rg_lru(x, w_a, b_a, w_x, b_x, lam):
    B, S, _ = x.shape
    xb = x.reshape(B, S, LRU_BLOCKS, LRU_BLOCK_DIM)
    r = jax.nn.sigmoid((jnp.einsum('bsgi,gij->bsgj', xb, w_a).reshape(B, S, D_RNN) + b_a).astype(jnp.float32))
    i = jax.nn.sigmoid((jnp.einsum('bsgi,gij->bsgj', xb, w_x).reshape(B, S, D_RNN) + b_x).astype(jnp.float32))
    log_a = -LRU_C * r * jax.nn.softplus(-lam.astype(jnp.float32))
    a = jnp.exp(log_a)
    u = jnp.sqrt(-jnp.expm1(2.0 * log_a)) * (i * x.astype(jnp.float32))

    def combine(left, right):
        a1, b1 = left
        a2, b2 = right
        return a1 * a2, a2 * b1 + b2

    _, h = lax.associative_scan(combine, (a, u), axis=1)
    return h.astype(x.dtype)


def _diff_attention(q, k, v, lam):
    B, S = q.shape[0], q.shape[1]
    nb = S // Q_BLOCK
    scale = HEAD_DIM ** -0.5
    kf = k.astype(jnp.float32)
    vf = v.astype(jnp.float32)
    qb = q.astype(jnp.float32).reshape(B, nb, Q_BLOCK, N_HEADS, 2, HEAD_DIM).transpose(1, 0, 2, 3, 4, 5)
    k_idx = jnp.arange(S)

    def block(args):
        q_blk, j = args
        s = jnp.einsum('bqhcd,bkhcd->bhcqk', q_blk, kf) * scale
        q_idx = j * Q_BLOCK + jnp.arange(Q_BLOCK)
        s = jnp.where(k_idx[None, :] <= q_idx[:, None], s, -jnp.inf)
        p = jax.nn.softmax(s, axis=-1)
        w = p[:, :, 0] - lam * p[:, :, 1]
        return jnp.einsum('bhqk,bkhe->bqhe', w, vf)

    o = lax.map(block, (qb, jnp.arange(nb)))
    return o.transpose(1, 0, 2, 3, 4).reshape(B, S, N_HEADS, V_DIM)


def _swiglu(h, w_gate, w_up, w_down):
    return (jax.nn.silu(h @ w_gate) * (h @ w_up)) @ w_down


def _moe_routed(h, w_router, router_bias, we_gate, we_up, we_down):
    T, D = h.shape
    scores = jax.nn.sigmoid(h.astype(jnp.float32) @ w_router.astype(jnp.float32))
    _, idx = lax.top_k(scores + router_bias.astype(jnp.float32), TOP_K)
    s_sel = jnp.take_along_axis(scores, idx, axis=-1)
    gates = s_sel / jnp.sum(s_sel, axis=-1, keepdims=True) * ROUTED_SCALE

    n_slots = T * TOP_K
    flat_e = idx.reshape(-1).astype(jnp.int32)
    flat_g = gates.reshape(-1)
    order = jnp.argsort(flat_e, stable=True)
    e_sorted = flat_e[order]
    tok_sorted = (order // TOP_K).astype(jnp.int32)
    g_sorted = flat_g[order]
    counts = jnp.bincount(flat_e, length=N_EXPERTS)
    padded = (counts + MOE_BLOCK - 1) // MOE_BLOCK * MOE_BLOCK
    starts = jnp.cumsum(counts) - counts
    padded_ends = jnp.cumsum(padded)
    padded_starts = padded_ends - padded
    dest = padded_starts[e_sorted] + jnp.arange(n_slots) - starts[e_sorted]
    cap = n_slots + N_EXPERTS * MOE_BLOCK
    n_blocks = cap // MOE_BLOCK
    buf_tok = jnp.full((cap,), T, jnp.int32).at[dest].set(tok_sorted)
    buf_g = jnp.zeros((cap,), jnp.float32).at[dest].set(g_sorted)
    block_e = jnp.minimum(
        jnp.searchsorted(padded_ends, jnp.arange(n_blocks) * MOE_BLOCK, side='right'),
        N_EXPERTS - 1).astype(jnp.int32)
    h_pad = jnp.concatenate([h, jnp.zeros((1, D), h.dtype)], axis=0)

    def step(acc, blk):
        tok, g, e = blk
        xb = h_pad[tok]
        yb = _swiglu(xb, we_gate[e], we_up[e], we_down[e])
        return acc.at[tok].add(yb.astype(jnp.float32) * g[:, None]), None

    acc, _ = lax.scan(step, jnp.zeros((T + 1, D), jnp.float32),
                      (buf_tok.reshape(n_blocks, MOE_BLOCK), buf_g.reshape(n_blocks, MOE_BLOCK), block_e))
    return acc[:T].astype(h.dtype)


def _layer(x, c, cos, sin, lambda_init, w_ada, b_ada, norm1_g, w_in, conv_w, conv_b,
           lru_wa, lru_ba, lru_wx, lru_bx, lru_lambda, q_norm_g, k_norm_g,
           lam_q1, lam_k1, lam_q2, lam_k2, subln_g, w_o, norm2_g, w_router, router_bias,
           we_gate, we_up, we_down, ws_gate, ws_up, ws_down):
    B, S, D = x.shape
    mod = jax.nn.silu(c) @ w_ada + b_ada
    shift1, scale1, gate1, shift2, scale2, gate2 = jnp.split(mod, 6, axis=-1)

    h = _rmsnorm(x, norm1_g) * (1.0 + scale1[:, None]) + shift1[:, None]
    proj = h @ w_in
    cuts = []
    off = 0
    for size in IN_SPLITS[:-1]:
        off += size
        cuts.append(off)
    x_rnn, g_rnn, q, k, v, g_a, g_b = jnp.split(proj, cuts, axis=-1)

    x_rnn = _causal_depthwise_conv(x_rnn, conv_w, conv_b)
    y_a = _rg_lru(x_rnn, lru_wa, lru_ba, lru_wx, lru_bx, lru_lambda) * jax.nn.gelu(g_rnn)

    q = _rotary_partial(_rmsnorm(q.reshape(B, S, N_HEADS, 2, HEAD_DIM), q_norm_g), cos, sin)
    k = _rotary_partial(_rmsnorm(k.reshape(B, S, N_HEADS, 2, HEAD_DIM), k_norm_g), cos, sin)
    v = v.reshape(B, S, N_HEADS, V_DIM)
    lam = (jnp.exp(jnp.sum(lam_q1.astype(jnp.float32) * lam_k1.astype(jnp.float32)))
           - jnp.exp(jnp.sum(lam_q2.astype(jnp.float32) * lam_k2.astype(jnp.float32)))
           + lambda_init)
    o = _diff_attention(q, k, v, lam)
    y_b = (_rmsnorm(o, subln_g) * (1.0 - lambda_init)).reshape(B, S, ATT_WIDTH).astype(x.dtype)

    mixed = jax.nn.sigmoid(g_a) * y_a + jax.nn.sigmoid(g_b) * y_b
    x = x + gate1[:, None] * (mixed @ w_o)

    h2 = (_rmsnorm(x, norm2_g) * (1.0 + scale2[:, None]) + shift2[:, None]).reshape(B * S, D)
    y = _swiglu(h2, ws_gate, ws_up, ws_down) + _moe_routed(h2, w_router, router_bias, we_gate, we_up, we_down)
    return x + gate2[:, None] * y.reshape(B, S, D)


def setup_inputs(seed: int = 0) -> dict:
    key = jax.random.key(seed)
    ks = jax.random.split(key, 32)
    f32 = jnp.float32
    L = DEPTH
    D = D_MODEL

    def nrm(k, shape, s):
        return jax.random.normal(k, shape, f32) * s

    u = jax.random.uniform(ks[12], (L, D_RNN), f32, 0.9, 0.999)
    base = u ** (1.0 / LRU_C)
    return {
        'x': nrm(ks[0], (BATCH, SEQ, D), 1.0),
        'c': nrm(ks[1], (BATCH, D), 1.0),
        'positions': jnp.tile(jnp.arange(SEQ, dtype=jnp.int32)[None, :], (BATCH, 1)),
        'w_ada': nrm(ks[2], (L, D, 6 * D), 0.5 * D ** -0.5),
        'b_ada': nrm(ks[3], (L, 6 * D), 0.01),
        'norm1_g': 1.0 + nrm(ks[4], (L, D), 0.01),
        'w_in': nrm(ks[5], (L, D, IN_WIDTH), D ** -0.5),
        'conv_w': nrm(ks[6], (L, CONV_WIDTH, D_RNN), CONV_WIDTH ** -0.5),
        'conv_b': nrm(ks[7], (L, D_RNN), 0.01),
        'lru_wa': nrm(ks[8], (L, LRU_BLOCKS, LRU_BLOCK_DIM, LRU_BLOCK_DIM), LRU_BLOCK_DIM ** -0.5),
        'lru_ba': nrm(ks[9], (L, D_RNN), 0.01),
        'lru_wx': nrm(ks[10], (L, LRU_BLOCKS, LRU_BLOCK_DIM, LRU_BLOCK_DIM), LRU_BLOCK_DIM ** -0.5),
        'lru_bx': nrm(ks[11], (L, D_RNN), 0.01),
        'lru_lambda': jnp.log(base) - jnp.log1p(-base),
        'q_norm_g': 1.0 + nrm(ks[13], (L, HEAD_DIM), 0.01),
        'k_norm_g': 1.0 + nrm(ks[14], (L, HEAD_DIM), 0.01),
        'lam_q1': nrm(ks[15], (L, HEAD_DIM), 0.1),
        'lam_k1': nrm(ks[16], (L, HEAD_DIM), 0.1),
        'lam_q2': nrm(ks[17], (L, HEAD_DIM), 0.1),
        'lam_k2': nrm(ks[18], (L, HEAD_DIM), 0.1),
        'subln_g': 1.0 + nrm(ks[19], (L, V_DIM), 0.01),
        'w_o': nrm(ks[20], (L, D, D), D ** -0.5),
        'norm2_g': 1.0 + nrm(ks[21], (L, D), 0.01),
        'w_router': nrm(ks[22], (L, D, N_EXPERTS), D ** -0.5),
        'router_bias': nrm(ks[23], (L, N_EXPERTS), 0.01),
        'we_gate': nrm(ks[24], (L, N_EXPERTS, D, D_EXPERT), D ** -0.5),
        'we_up': nrm(ks[25], (L, N_EXPERTS, D, D_EXPERT), D ** -0.5),
        'we_down': nrm(ks[26], (L, N_EXPERTS, D_EXPERT, D), D_EXPERT ** -0.5),
        'ws_gate': nrm(ks[27], (L, D, D_SHARED), D ** -0.5),
        'ws_up': nrm(ks[28], (L, D, D_SHARED), D ** -0.5),
        'ws_down': nrm(ks[29], (L, D_SHARED, D), D_SHARED ** -0.5),
    }


def reference(x, c, positions, w_ada, b_ada, norm1_g, w_in, conv_w, conv_b,
              lru_wa, lru_ba, lru_wx, lru_bx, lru_lambda, q_norm_g, k_norm_g,
              lam_q1, lam_k1, lam_q2, lam_k2, subln_g, w_o, norm2_g, w_router, router_bias,
              we_gate, we_up, we_down, ws_gate, ws_up, ws_down):
    inv_freq = ROPE_THETA ** (-jnp.arange(0, ROT_DIM, 2, dtype=jnp.float32) / ROT_DIM)
    ang = positions.astype(jnp.float32)[..., None] * inv_freq
    cos = jnp.cos(ang)[:, :, None, None, :]
    sin = jnp.sin(ang)[:, :, None, None, :]
    for l in range(DEPTH):
        lambda_init = 0.8 - 0.6 * math.exp(-0.3 * l)
        x = _layer(x, c, cos, sin, lambda_init, w_ada[l], b_ada[l], norm1_g[l], w_in[l],
                   conv_w[l], conv_b[l], lru_wa[l], lru_ba[l], lru_wx[l], lru_bx[l], lru_lambda[l],
                   q_norm_g[l], k_norm_g[l], lam_q1[l], lam_k1[l], lam_q2[l], lam_k2[l], subln_g[l],
                   w_o[l], norm2_g[l], w_router[l], router_bias[l],
                   we_gate[l], we_up[l], we_down[l], ws_gate[l], ws_up[l], ws_down[l])
    return x
```

```python
import functools
import math

import jax
import jax.numpy as jnp
from jax import lax
from jax.experimental import pallas as pl
from jax.experimental.pallas import tpu as pltpu

F32 = jnp.float32
BF16 = jnp.bfloat16
I32 = jnp.int32

D_MODEL = 1024
N_HEADS = 8
HEAD_DIM = 64
V_DIM = 128
ROT_DIM = 16
ROPE_THETA = 500000.0
LRU_C = 8.0
LRU_BLOCK_DIM = 64
LRU_GROUP = 256
N_EXPERTS = 256
TOP_K = 8
D_EXPERT = 256
D_SHARED = 256
ROUTED_SCALE = 2.5
EPS = 1e-6
N_PROJ = 7
LOG2E = 1.4426950408889634

LANES = 128
SUBLANES = 8
VMEM_LIMIT = 48 * 1024 * 1024


def _cparams(*sem):
    return pltpu.CompilerParams(dimension_semantics=sem, vmem_limit_bytes=VMEM_LIMIT)


def _sigmoid(x):
    return jax.nn.sigmoid(x)


def _adaln_kernel(c_ref, w_ref, b_ref, o_ref):
    c = c_ref[...]
    sc = c * _sigmoid(c)
    o_ref[...] = jnp.dot(sc, w_ref[...], preferred_element_type=F32,
                         precision=lax.Precision.HIGHEST) + b_ref[...]


def _adaln(c, w_ada, b_ada):
    bsz, d = c.shape
    n = w_ada.shape[1]
    tn = d
    return pl.pallas_call(
        _adaln_kernel,
        grid=(n // tn,),
        in_specs=[pl.BlockSpec((bsz, d), lambda j: (0, 0)),
                  pl.BlockSpec((d, tn), lambda j: (0, j)),
                  pl.BlockSpec((1, tn), lambda j: (0, j))],
        out_specs=pl.BlockSpec((bsz, tn), lambda j: (0, j)),
        out_shape=jax.ShapeDtypeStruct((bsz, n), F32),
        compiler_params=_cparams("arbitrary"),
        name="adaln",
    )(c, w_ada, b_ada.reshape(1, n))


def _modulated_rmsnorm(x, g, shift, scale):
    ms = jnp.mean(x * x, axis=-1, keepdims=True)
    return (x * lax.rsqrt(ms + EPS) * g) * (1.0 + scale) + shift


def _inproj_kernel(x_ref, mod_ref, g_ref, w_ref, *o_refs):
    h = _modulated_rmsnorm(x_ref[...], g_ref[...], mod_ref[0, 0:1, :], mod_ref[0, 1:2, :])
    hb = h.astype(BF16)
    d = x_ref.shape[1]
    for j, o_ref in enumerate(o_refs):
        o_ref[...] = jnp.dot(hb, w_ref[:, j * d:(j + 1) * d],
                             preferred_element_type=F32).astype(o_ref.dtype)


def _inproj(x2, mod3, g1, w_in_bf, seq, tm):
    t, d = x2.shape
    tiles_per_seq = seq // tm
    row = pl.BlockSpec((tm, d), lambda i: (i, 0))
    return pl.pallas_call(
        _inproj_kernel,
        grid=(t // tm,),
        in_specs=[row,
                  pl.BlockSpec((1, 6, d), lambda i: (i // tiles_per_seq, 0, 0)),
                  pl.BlockSpec((1, d), lambda i: (0, 0)),
                  pl.BlockSpec((d, N_PROJ * d), lambda i: (0, 0), pipeline_mode=pl.Buffered(1))],
        out_specs=[row] * N_PROJ,
        out_shape=[jax.ShapeDtypeStruct((t, d), BF16)] * N_PROJ,
        compiler_params=_cparams("arbitrary"),
        name="inproj",
    )(x2, mod3, g1, w_in_bf)


def _rglru_kernel(x_ref, g_ref, cw_ref, cb_ref, wa_ref, ba_ref, wx_ref, bx_ref, lam_ref,
                  o_ref, xbuf, a_s, u_s, h_s, hc_s):
    ts, d = x_ref.shape
    s = pl.program_id(1)

    @pl.when(s == 0)
    def _():
        xbuf[0:SUBLANES, :] = jnp.zeros((SUBLANES, d), F32)
        hc_s[...] = jnp.zeros_like(hc_s)

    xbuf[SUBLANES:, :] = x_ref[...].astype(F32)
    xc = cb_ref[...] + cw_ref[3:4, :] * xbuf[pl.ds(SUBLANES, ts), :]
    for back in (1, 2, 3):
        xc = xc + cw_ref[3 - back:4 - back, :] * xbuf[pl.ds(SUBLANES - back, ts), :]
    tail = xbuf[pl.ds(ts, SUBLANES), :]
    xbuf[0:SUBLANES, :] = tail

    xcb = xc.astype(BF16)
    rs, is_ = [], []
    for gi in range(d // LRU_GROUP):
        sl = slice(gi * LRU_GROUP, (gi + 1) * LRU_GROUP)
        rs.append(jnp.dot(xcb[:, sl], wa_ref[gi], preferred_element_type=F32))
        is_.append(jnp.dot(xcb[:, sl], wx_ref[gi], preferred_element_type=F32))
    r = _sigmoid(jnp.concatenate(rs, axis=1) + ba_ref[...])
    ig = _sigmoid(jnp.concatenate(is_, axis=1) + bx_ref[...])
    neg_lam = -lam_ref[...]
    softplus = jnp.maximum(neg_lam, 0.0) + jnp.log1p(jnp.exp(-jnp.abs(neg_lam)))
    log_a = (-LRU_C) * r * softplus
    a = jnp.exp(log_a)
    u = jnp.sqrt(1.0 - a * a) * (ig * xc)

    rmod = lax.broadcasted_iota(I32, (ts, d), 0) & (SUBLANES - 1)
    for step in (1, 2, 4):
        keep = rmod >= step
        a_prev = jnp.where(keep, pltpu.roll(a, step, 0), 1.0)
        u_prev = jnp.where(keep, pltpu.roll(u, step, 0), 0.0)
        u = u + a * u_prev
        a = a * a_prev
    a_s[...] = a
    u_s[...] = u

    def carry_body(c, hc):
        rows = pl.ds(pl.multiple_of(c * SUBLANES, SUBLANES), SUBLANES)
        h = u_s[rows, :] + a_s[rows, :] * hc
        h_s[rows, :] = h
        return h[SUBLANES - 1:SUBLANES, :]

    hc_s[...] = lax.fori_loop(0, ts // SUBLANES, carry_body, hc_s[...])
    o_ref[...] = (h_s[...] * jax.nn.gelu(g_ref[...].astype(F32))).astype(o_ref.dtype)


def _rglru(xr, gr, conv_w, conv_b, wa4, ba, wx4, bx, lam, bsz, seq, ts):
    t, d = xr.shape
    ns = seq // ts
    row = pl.BlockSpec((ts, d), lambda b, s: (b * ns + s, 0))
    vec = pl.BlockSpec((1, d), lambda b, s: (0, 0))
    wsp = pl.BlockSpec(wa4.shape, lambda b, s: (0, 0, 0))
    return pl.pallas_call(
        _rglru_kernel,
        grid=(bsz, ns),
        in_specs=[row, row, pl.BlockSpec((4, d), lambda b, s: (0, 0)), vec, wsp, vec, wsp, vec, vec],
        out_specs=row,
        out_shape=jax.ShapeDtypeStruct((t, d), BF16),
        scratch_shapes=[pltpu.VMEM((ts + SUBLANES, d), F32), pltpu.VMEM((ts, d), F32),
                        pltpu.VMEM((ts, d), F32), pltpu.VMEM((ts, d), F32), pltpu.VMEM((1, d), F32)],
        compiler_params=_cparams("arbitrary", "arbitrary"),
        name="rglru",
    )(xr, gr, conv_w, conv_b, wa4, ba, wx4, bx, lam)


def _qkprep_kernel(q_ref, k_ref, pos_ref, qg_ref, kg_ref, invf_ref, ones_ref, qo_ref, ko_ref, *, q_scale):
    ts, d = q_ref.shape
    ang = pos_ref[...] * invf_ref[...]
    cos = jnp.cos(ang)
    sin = jnp.sin(ang)
    lane = lax.broadcasted_iota(I32, (ts, LANES), 1) & (HEAD_DIM - 1)
    first_half = lane < (ROT_DIM // 2)
    sin_signed = jnp.where(first_half, -sin, sin)
    blk = ones_ref[...]

    def prep(x_ref, g_ref, o_ref, scale):
        for j in range(d // LANES):
            sl = slice(j * LANES, (j + 1) * LANES)
            x = x_ref[:, sl].astype(F32)
            x2 = x * x
            hi = x2.astype(BF16)
            lo = (x2 - hi.astype(F32)).astype(BF16)
            ssq = (jnp.dot(hi, blk, preferred_element_type=F32)
                   + jnp.dot(lo, blk, preferred_element_type=F32))
            y = x * lax.rsqrt(ssq * (1.0 / HEAD_DIM) + EPS) * g_ref[:, sl]
            partner = jnp.where(first_half, pltpu.roll(y, LANES - ROT_DIM // 2, 1),
                                pltpu.roll(y, ROT_DIM // 2, 1))
            o_ref[:, sl] = ((y * cos + partner * sin_signed) * scale).astype(o_ref.dtype)

    prep(q_ref, qg_ref, qo_ref, q_scale)
    prep(k_ref, kg_ref, ko_ref, 1.0)


def _qkprep(q, k, posf, qg, kg, invf, ones_blk, ts):
    t, d = q.shape
    row = pl.BlockSpec((ts, d), lambda i: (i, 0))
    vec = pl.BlockSpec((1, d), lambda i: (0, 0))
    q_scale = HEAD_DIM ** -0.5 * LOG2E
    return pl.pallas_call(
        functools.partial(_qkprep_kernel, q_scale=q_scale),
        grid=(t // ts,),
        in_specs=[row, row, pl.BlockSpec((ts, 1), lambda i: (i, 0)), vec, vec,
                  pl.BlockSpec((1, LANES), lambda i: (0, 0)),
                  pl.BlockSpec((LANES, LANES), lambda i: (0, 0))],
        out_specs=[row, row],
        out_shape=[jax.ShapeDtypeStruct((t, d), BF16)] * 2,
        compiler_params=_cparams("arbitrary"),
        name="qkprep",
    )(q, k, posf, qg, kg, invf, ones_blk)


def _attn_kernel(lam_ref, q_ref, k_ref, v_ref, g_ref, o_ref, *, tk, lambda_init):
    tq = q_ref.shape[0]
    i = pl.program_id(2)
    lp = lam_ref[...]
    lam = (jnp.exp(jnp.sum(lp[0:1, :] * lp[1:2, :], axis=1, keepdims=True))
           - jnp.exp(jnp.sum(lp[2:3, :] * lp[3:4, :], axis=1, keepdims=True)) + lambda_init)

    q = q_ref[...]
    lane = lax.broadcasted_iota(I32, (tq, LANES), 1)
    zero = jnp.zeros_like(q)
    q2 = jnp.concatenate([jnp.where(lane < HEAD_DIM, q, zero), jnp.where(lane >= HEAD_DIM, q, zero)], axis=0)
    rows = 2 * tq
    qpos = i * tq + (lax.broadcasted_iota(I32, (rows, tk), 0) & (tq - 1))
    col = lax.broadcasted_iota(I32, (rows, tk), 1)
    n_kv = ((i + 1) * tq + tk - 1) // tk

    def body(j, carry):
        m, l, acc = carry
        start = pl.multiple_of(j * tk, tk)
        kj = k_ref[pl.ds(start, tk), :]
        vj = v_ref[pl.ds(start, tk), :]
        s = lax.dot_general(q2, kj, (((1,), (1,)), ((), ())), preferred_element_type=F32)
        s = jnp.where(col + j * tk <= qpos, s, -jnp.inf)
        m_new = jnp.maximum(m, jnp.max(s, axis=1, keepdims=True))
        alpha = jnp.exp2(m - m_new)
        p = jnp.exp2(s - m_new)
        l = alpha * l + jnp.sum(p, axis=1, keepdims=True)
        acc = alpha * acc + jnp.dot(p.astype(BF16), vj, preferred_element_type=F32)
        return m_new, l, acc

    m0 = jnp.full((rows, 1), -jnp.inf, F32)
    l0 = jnp.zeros((rows, 1), F32)
    acc0 = jnp.zeros((rows, V_DIM), F32)
    _, l, acc = lax.fori_loop(0, n_kv, body, (m0, l0, acc0))
    o_all = acc / l
    o = o_all[:tq, :] - lam * o_all[tq:, :]
    ms = jnp.mean(o * o, axis=-1, keepdims=True)
    o_ref[...] = (o * lax.rsqrt(ms + EPS) * g_ref[...] * (1.0 - lambda_init)).astype(o_ref.dtype)


def _attention(lamp, qh, kh, v, subln_g, bsz, seq, tq, tk, lambda_init):
    t, d = qh.shape
    nq = seq // tq
    qspec = pl.BlockSpec((tq, LANES), lambda b, h, i: (b * nq + i, h))
    kvspec = pl.BlockSpec((seq, LANES), lambda b, h, i: (b, h))
    return pl.pallas_call(
        functools.partial(_attn_kernel, tk=tk, lambda_init=lambda_init),
        grid=(bsz, N_HEADS, nq),
        in_specs=[pl.BlockSpec((4, HEAD_DIM), lambda b, h, i: (0, 0)), qspec, kvspec, kvspec,
                  pl.BlockSpec((1, V_DIM), lambda b, h, i: (0, 0))],
        out_specs=qspec,
        out_shape=jax.ShapeDtypeStruct((t, d), BF16),
        compiler_params=_cparams("arbitrary", "arbitrary", "arbitrary"),
        name="attn",
    )(lamp, qh, kh, v, subln_g)


def _mixproj_kernel(ya_ref, yb_ref, ga_ref, gb_ref, x_ref, mod_ref, g2_ref, wo_ref, wrh_ref, wrl_ref,
                    x1_ref, h2_ref, lg_ref):
    mixed = (_sigmoid(ga_ref[...].astype(F32)) * ya_ref[...].astype(F32)
             + _sigmoid(gb_ref[...].astype(F32)) * yb_ref[...].astype(F32))
    att = jnp.dot(mixed.astype(BF16), wo_ref[...], preferred_element_type=F32)
    x1 = x_ref[...] + mod_ref[0, 2:3, :] * att
    x1_ref[...] = x1
    h2 = _modulated_rmsnorm(x1, g2_ref[...], mod_ref[0, 3:4, :], mod_ref[0, 4:5, :])
    h2_ref[...] = h2
    hi = h2.astype(BF16)
    lo = (h2 - hi.astype(F32)).astype(BF16)
    lg_ref[...] = (jnp.dot(hi, wrh_ref[...], preferred_element_type=F32)
                   + (jnp.dot(hi, wrl_ref[...], preferred_element_type=F32)
                      + jnp.dot(lo, wrh_ref[...], preferred_element_type=F32)))


def _mixproj(ya, yb, ga, gb, x2, mod3, g2, wo_bf, wr_hi, wr_lo, seq, tm):
    t, d = x2.shape
    tiles_per_seq = seq // tm
    row = pl.BlockSpec((tm, d), lambda i: (i, 0))
    full = lambda shape: pl.BlockSpec(shape, lambda i: (0,) * len(shape))
    return pl.pallas_call(
        _mixproj_kernel,
        grid=(t // tm,),
        in_specs=[row, row, row, row, row,
                  pl.BlockSpec((1, 6, d), lambda i: (i // tiles_per_seq, 0, 0)),
                  full((1, d)), full((d, d)), full((d, N_EXPERTS)), full((d, N_EXPERTS))],
        out_specs=[row, row, pl.BlockSpec((tm, N_EXPERTS), lambda i: (i, 0))],
        out_shape=[jax.ShapeDtypeStruct((t, d), F32), jax.ShapeDtypeStruct((t, d), F32),
                   jax.ShapeDtypeStruct((t, N_EXPERTS), F32)],
        compiler_params=_cparams("arbitrary"),
        name="mixproj",
    )(ya, yb, ga, gb, x2, mod3, g2, wo_bf, wr_hi, wr_lo)


def _route_kernel(lg_ref, bias_ref, idx_ref, rank_ref, gate_ref, cnt_ref, carry_s):
    tm = lg_ref.shape[0]
    i = pl.program_id(0)

    @pl.when(i == 0)
    def _():
        carry_s[...] = jnp.zeros_like(carry_s)

    scores = _sigmoid(lg_ref[...].T)
    sel = scores + bias_ref[...]
    erow = lax.broadcasted_iota(I32, (N_EXPERTS, tm), 0)
    chosen = jnp.zeros((N_EXPERTS, tm), F32)
    idxs, gates = [], []
    for _ in range(TOP_K):
        mx = jnp.max(sel, axis=0, keepdims=True)
        idx = jnp.min(jnp.where(sel == mx, erow, N_EXPERTS), axis=0, keepdims=True)
        hit = erow == idx
        gates.append(jnp.sum(jnp.where(hit, scores, 0.0), axis=0, keepdims=True))
        sel = jnp.where(hit, -jnp.inf, sel)
        chosen = jnp.where(hit, 1.0, chosen)
        idxs.append(idx)
    gsum = gates[0]
    for g in gates[1:]:
        gsum = gsum + g
    gate_rows = [g / gsum * ROUTED_SCALE for g in gates]

    r_i = lax.broadcasted_iota(I32, (tm, tm), 0)
    c_i = lax.broadcasted_iota(I32, (tm, tm), 1)
    earlier = jnp.where(r_i < c_i, 1.0, 0.0).astype(BF16)
    before = jnp.dot(chosen.astype(BF16), earlier, preferred_element_type=F32) + carry_s[...]
    ranks = [jnp.sum(jnp.where(erow == idx, before, 0.0), axis=0, keepdims=True) for idx in idxs]
    carry_s[...] = carry_s[...] + jnp.sum(chosen, axis=1, keepdims=True)
    cnt_ref[...] = carry_s[...]

    idx_ref[...] = jnp.concatenate(idxs, axis=0)
    rank_ref[...] = jnp.concatenate(ranks, axis=0).astype(I32)
    gpad = jnp.concatenate(gate_rows + [jnp.zeros((LANES - TOP_K, tm), F32)], axis=0)
    gate_ref[...] = gpad.T


def _route(logits, bias_col, tm):
    t = logits.shape[0]
    krow = pl.BlockSpec((TOP_K, tm), lambda i: (0, i))
    return pl.pallas_call(
        _route_kernel,
        grid=(t // tm,),
        in_specs=[pl.BlockSpec((tm, N_EXPERTS), lambda i: (i, 0)),
                  pl.BlockSpec((N_EXPERTS, 1), lambda i: (0, 0))],
        out_specs=[krow, krow, pl.BlockSpec((tm, LANES), lambda i: (i, 0)),
                   pl.BlockSpec((N_EXPERTS, 1), lambda i: (0, 0))],
        out_shape=[jax.ShapeDtypeStruct((TOP_K, t), I32), jax.ShapeDtypeStruct((TOP_K, t), I32),
                   jax.ShapeDtypeStruct((t, LANES), F32), jax.ShapeDtypeStruct((N_EXPERTS, 1), F32)],
        scratch_shapes=[pltpu.VMEM((N_EXPERTS, 1), F32)],
        compiler_params=_cparams("arbitrary"),
        name="route",
    )(logits, bias_col)


def _dest_kernel(idx_ref, rank_ref, start_ref, dest_ref):
    tm = idx_ref.shape[1]
    erow = lax.broadcasted_iota(I32, (N_EXPERTS, tm), 0)
    start = start_ref[...]
    rows = []
    for k in range(TOP_K):
        hit = erow == idx_ref[k:k + 1, :]
        rows.append(jnp.sum(jnp.where(hit, start, 0), axis=0, keepdims=True))
    dest_ref[...] = jnp.concatenate(rows, axis=0) + rank_ref[...]


def _dest(idx_t, rank_t, start_col, tm):
    t = idx_t.shape[1]
    krow = pl.BlockSpec((TOP_K, tm), lambda i: (0, i))
    return pl.pallas_call(
        _dest_kernel,
        grid=(t // tm,),
        in_specs=[krow, krow, pl.BlockSpec((N_EXPERTS, 1), lambda i: (0, 0))],
        out_specs=krow,
        out_shape=jax.ShapeDtypeStruct((TOP_K, t), I32),
        compiler_params=_cparams("arbitrary"),
        name="dest",
    )(idx_t, rank_t, start_col)


def _dispatch_kernel(dest_ref, h_ref, xs_in_ref, xs_ref, sem):
    del xs_in_ref
    tt = h_ref.shape[0]

    def row_copy(t, k):
        return pltpu.make_async_copy(h_ref.at[pl.ds(t, 1), :], xs_ref.at[pl.ds(dest_ref[k, t], 1), :], sem)

    def issue(t, _):
        for k in range(TOP_K):
            row_copy(t, k).start()
        return 0

    lax.fori_loop(0, tt, issue, 0)
    for _ in range(TOP_K):
        pltpu.make_async_copy(h_ref, xs_ref.at[pl.ds(0, tt), :], sem).wait()


def _dispatch(dest_t, h2, xs_zero, tt):
    t, d = h2.shape
    return pl.pallas_call(
        _dispatch_kernel,
        grid=(t // tt,),
        in_specs=[pl.BlockSpec((TOP_K, tt), lambda i: (0, i), memory_space=pltpu.SMEM),
                  pl.BlockSpec((tt, d), lambda i: (i, 0)),
                  pl.BlockSpec(memory_space=pl.ANY)],
        out_specs=pl.BlockSpec(memory_space=pl.ANY),
        out_shape=jax.ShapeDtypeStruct(xs_zero.shape, xs_zero.dtype),
        scratch_shapes=[pltpu.SemaphoreType.DMA(())],
        input_output_aliases={2: 0},
        compiler_params=_cparams("arbitrary"),
        name="dispatch",
    )(dest_t, h2, xs_zero)


def _experts_kernel(be_ref, nv_ref, x_ref, wg_ref, wu_ref, wd_ref, y_ref):
    j = pl.program_id(0)

    @pl.when(j < nv_ref[0])
    def _():
        xb = x_ref[...].astype(BF16)
        g = jnp.dot(xb, wg_ref[0].astype(BF16), preferred_element_type=F32)
        u = jnp.dot(xb, wu_ref[0].astype(BF16), preferred_element_type=F32)
        hmid = (g * _sigmoid(g) * u).astype(BF16)
        y_ref[...] = jnp.dot(hmid, wd_ref[0].astype(BF16), preferred_element_type=F32)

    @pl.when(j >= nv_ref[0])
    def _():
        y_ref[...] = jnp.zeros_like(y_ref)


def _experts(block_e, n_valid, xs, we_gate, we_up, we_down, tb):
    cap, d = xs.shape
    n_blocks = cap // tb

    def rows(j, be, nv):
        return (jnp.minimum(j, nv[0] - 1), 0)

    def wsel(j, be, nv):
        return (be[jnp.minimum(j, nv[0] - 1)], 0, 0)

    grid_spec = pltpu.PrefetchScalarGridSpec(
        num_scalar_prefetch=2,
        grid=(n_blocks,),
        in_specs=[pl.BlockSpec((tb, d), rows),
                  pl.BlockSpec((1, d, D_EXPERT), wsel),
                  pl.BlockSpec((1, d, D_EXPERT), wsel),
                  pl.BlockSpec((1, D_EXPERT, d), wsel)],
        out_specs=pl.BlockSpec((tb, d), lambda j, be, nv: (j, 0)),
    )
    return pl.pallas_call(
        _experts_kernel,
        grid_spec=grid_spec,
        out_shape=jax.ShapeDtypeStruct((cap, d), F32),
        compiler_params=_cparams("arbitrary"),
        name="experts",
    )(block_e, n_valid, xs, we_gate, we_up, we_down)


def _combine_kernel(dest_ref, gate_ref, h_ref, x1_ref, mod_ref, wsg_ref, wsu_ref, wsd_ref, ys_ref,
                    o_ref, buf, sem):
    tt = h_ref.shape[0]

    def issue(t, _):
        for k in range(TOP_K):
            pltpu.make_async_copy(ys_ref.at[pl.ds(dest_ref[k, t], 1), :], buf.at[k, pl.ds(t, 1), :], sem).start()
        return 0

    lax.fori_loop(0, tt, issue, 0)

    hb = h_ref[...].astype(BF16)
    g = jnp.dot(hb, wsg_ref[...], preferred_element_type=F32)
    u = jnp.dot(hb, wsu_ref[...], preferred_element_type=F32)
    y = jnp.dot((g * _sigmoid(g) * u).astype(BF16), wsd_ref[...], preferred_element_type=F32)

    for k in range(TOP_K):
        pltpu.make_async_copy(ys_ref.at[pl.ds(0, tt), :], buf.at[k], sem).wait()
    gates = gate_ref[...]
    for k in range(TOP_K):
        y = y + gates[:, k:k + 1] * buf[k]
    o_ref[...] = x1_ref[...] + mod_ref[0, 5:6, :] * y


def _combine(dest_t, gate128, h2, x1, mod3, wsg, wsu, wsd, ys, seq, tt):
    t, d = h2.shape
    tiles_per_seq = seq // tt
    row = pl.BlockSpec((tt, d), lambda i: (i, 0))
    full = lambda shape: pl.BlockSpec(shape, lambda i: (0,) * len(shape))
    return pl.pallas_call(
        _combine_kernel,
        grid=(t // tt,),
        in_specs=[pl.BlockSpec((TOP_K, tt), lambda i: (0, i), memory_space=pltpu.SMEM),
                  pl.BlockSpec((tt, LANES), lambda i: (i, 0)), row, row,
                  pl.BlockSpec((1, 6, d), lambda i: (i // tiles_per_seq, 0, 0)),
                  full((d, D_SHARED)), full((d, D_SHARED)), full((D_SHARED, d)),
                  pl.BlockSpec(memory_space=pl.ANY)],
        out_specs=row,
        out_shape=jax.ShapeDtypeStruct((t, d), F32),
        scratch_shapes=[pltpu.VMEM((TOP_K, tt, d), F32), pltpu.SemaphoreType.DMA(())],
        compiler_params=_cparams("arbitrary"),
        name="combine",
    )(dest_t, gate128, h2, x1, mod3, wsg, wsu, wsd, ys)


def _tiles(seq):
    return dict(
        proj=min(512, seq),
        scan=min(512, seq),
        prep=min(512, seq),
        tq=min(256, seq), tk=min(512, seq),
        route=min(512, seq),
        move=min(256, seq),
        expert=256,
    )


def _block_diag_groups(w):
    nb, bd, _ = w.shape
    per = LRU_GROUP // bd
    eye = jnp.eye(per, dtype=w.dtype)
    w5 = w.reshape(nb // per, per, bd, 1, bd) * eye[None, :, None, :, None]
    return w5.reshape(nb // per, LRU_GROUP, LRU_GROUP).astype(BF16)


def _layer(x2, c, posf, invf, ones_blk, lambda_init, bsz, seq, w_ada, b_ada, norm1_g, w_in, conv_w, conv_b,
           lru_wa, lru_ba, lru_wx, lru_bx, lru_lambda, q_norm_g, k_norm_g, lam_q1, lam_k1, lam_q2, lam_k2,
           subln_g, w_o, norm2_g, w_router, router_bias, we_gate, we_up, we_down, ws_gate, ws_up, ws_down):
    t, d = x2.shape
    tl = _tiles(seq)
    vec = lambda a: a.reshape(1, -1)

    mod3 = _adaln(c, w_ada, b_ada).reshape(bsz, 6, d)
    xr, gr, q, k, v, ga, gb = _inproj(x2, mod3, vec(norm1_g), w_in.astype(BF16), seq, tl["proj"])

    ya = _rglru(xr, gr, conv_w, vec(conv_b), _block_diag_groups(lru_wa), vec(lru_ba),
                _block_diag_groups(lru_wx), vec(lru_bx), vec(lru_lambda), bsz, seq, tl["scan"])

    reps = d // HEAD_DIM
    qh, kh = _qkprep(q, k, posf, vec(jnp.tile(q_norm_g, reps)), vec(jnp.tile(k_norm_g, reps)),
                     invf, ones_blk, tl["prep"])
    lamp = jnp.stack([lam_q1, lam_k1, lam_q2, lam_k2]).astype(F32)
    yb = _attention(lamp, qh, kh, v, vec(subln_g), bsz, seq, tl["tq"], tl["tk"], lambda_init)

    wr_hi = w_router.astype(BF16)
    wr_lo = (w_router - wr_hi.astype(F32)).astype(BF16)
    x1, h2, logits = _mixproj(ya, yb, ga, gb, x2, mod3, vec(norm2_g), w_o.astype(BF16), wr_hi, wr_lo,
                              seq, tl["proj"])

    idx_t, rank_t, gate128, counts = _route(logits, router_bias.reshape(N_EXPERTS, 1), tl["route"])

    tb = tl["expert"]
    cnt = counts.reshape(N_EXPERTS).astype(I32)
    padded = (cnt + tb - 1) // tb * tb
    ends = jnp.cumsum(padded)
    starts = ends - padded
    cap = t * TOP_K + N_EXPERTS * tb
    n_blocks = cap // tb
    block_e = jnp.minimum(jnp.searchsorted(ends, jnp.arange(n_blocks, dtype=I32) * tb, side="right"),
                          N_EXPERTS - 1).astype(I32)
    n_valid = (ends[-1:] // tb).astype(I32)

    dest_t = _dest(idx_t, rank_t, starts.reshape(N_EXPERTS, 1), tl["route"])
    xs = _dispatch(dest_t, h2, jnp.zeros((cap, d), F32), tl["move"])
    ys = _experts(block_e, n_valid, xs, we_gate, we_up, we_down, tb)
    return _combine(dest_t, gate128, h2, x1, mod3, ws_gate.astype(BF16), ws_up.astype(BF16),
                    ws_down.astype(BF16), ys, seq, tl["move"])


def kernel(x, c, positions, w_ada, b_ada, norm1_g, w_in, conv_w, conv_b, lru_wa, lru_ba, lru_wx, lru_bx,
           lru_lambda, q_norm_g, k_norm_g, lam_q1, lam_k1, lam_q2, lam_k2, subln_g, w_o, norm2_g, w_router,
           router_bias, we_gate, we_up, we_down, ws_gate, ws_up, ws_down):
    bsz, seq, d = x.shape
    assert d == D_MODEL and seq % 256 == 0
    depth = w_ada.shape[0]
    x2 = x.reshape(bsz * seq, d)
    posf = positions.astype(F32).reshape(bsz * seq, 1)

    inv_freq = ROPE_THETA ** (-jnp.arange(0, ROT_DIM, 2, dtype=F32) / ROT_DIM)
    lane = jnp.arange(LANES) % HEAD_DIM
    invf = jnp.where(lane < ROT_DIM, inv_freq[lane % (ROT_DIM // 2)], 0.0).reshape(1, LANES).astype(F32)
    blk_id = jnp.arange(LANES) // HEAD_DIM
    ones_blk = (blk_id[:, None] == blk_id[None, :]).astype(BF16)

    for l in range(depth):
        lambda_init = 0.8 - 0.6 * math.exp(-0.3 * l)
        x2 = _layer(x2, c, posf, invf, ones_blk, lambda_init, bsz, seq, w_ada[l], b_ada[l], norm1_g[l], w_in[l],
                    conv_w[l], conv_b[l], lru_wa[l], lru_ba[l], lru_wx[l], lru_bx[l], lru_lambda[l],
                    q_norm_g[l], k_norm_g[l], lam_q1[l], lam_k1[l], lam_q2[l], lam_k2[l], subln_g[l],
                    w_o[l], norm2_g[l], w_router[l], router_bias[l], we_gate[l], we_up[l], we_down[l],
                    ws_gate[l], ws_up[l], ws_down[l])
    return x2.reshape(bsz, seq, d)
```
